```python
import math
import jax, jax.numpy as jnp
from jax import lax
import numpy as np

D_MODEL = 1024
BATCH = 2
SEQ = 8192
DEPTH = 4

N_MIXERS = 4
HEAD_DIM = 64
N_HEADS = D_MODEL // HEAD_DIM
DIFF_HEADS = N_HEADS // 2
MEM_LEN = 256
X_HEADS = 4
X_WIDTH = X_HEADS * HEAD_DIM
MIX_OUT = D_MODEL + X_WIDTH
D_FF = -(-8 * D_MODEL // (3 * 256)) * 256
Q_BLOCK = 128
NUM_BUCKETS = 32
MAX_DISTANCE = 128
EPS = 1e-6
NEG_INF = -1e30
SCALE = HEAD_DIM ** -0.5
NSA_KV_GROUPS = 4
NSA_HPG = N_HEADS // NSA_KV_GROUPS
CMP_BLOCK = 32
CMP_STRIDE = 16
CMP_HIDDEN = 256
SLC_BLOCK = 64
SLC_TOPN = 16
WINDOW = 512
FORCED_SCORE = 1e4
MOBA_BLOCK = 256
MOBA_TOPK = 3
MOBA_QBLOCK = 64
SB_IN = 3 * D_MODEL + X_WIDTH
DIFF_IN = 3 * D_MODEL + X_WIDTH
NSA_IN = D_MODEL + 6 * NSA_KV_GROUPS * HEAD_DIM + 3 * N_HEADS + X_WIDTH
MOBA_IN = 3 * D_MODEL + X_WIDTH

kernel_name = "hybrid_sb_diff_nsa_moba_trunk"


def rms_norm(x, g):
    xf = x.astype(jnp.float32)
    y = xf * lax.rsqrt(jnp.mean(xf * xf, axis=-1, keepdims=True) + EPS)
    return (y * g.astype(jnp.float32)).astype(x.dtype)


def split_cols(t, widths):
    offs = np.cumsum([0] + list(widths))
    return [t[..., int(offs[a]):int(offs[a + 1])] for a in range(len(widths))]


def to_heads(t, n):
    B, S, _ = t.shape
    return t.reshape(B, S, n, -1).transpose(0, 2, 1, 3)


def from_heads(t):
    B, H, S, d = t.shape
    return t.transpose(0, 2, 1, 3).reshape(B, S, H * d)


def masked_softmax(s, mask):
    s = jnp.where(mask, s, NEG_INF)
    m = jnp.max(s, axis=-1, keepdims=True)
    p = jnp.exp(s - m) * mask
    return p / jnp.maximum(jnp.sum(p, axis=-1, keepdims=True), 1e-30)


def t5_bucket(dist):
    max_exact = NUM_BUCKETS // 2
    large = max_exact + (jnp.log(jnp.maximum(dist, 1).astype(jnp.float32) / max_exact)
                         / math.log(MAX_DISTANCE / max_exact) * (NUM_BUCKETS - max_exact)).astype(jnp.int32)
    large = jnp.minimum(large, NUM_BUCKETS - 1)
    return jnp.where(dist < max_exact, dist, large)


def swiglu(h, w_gate, w_up, w_down):
    return (jax.nn.silu(h @ w_gate) * (h @ w_up)) @ w_down


def cross_attention(xq, mem, mem_g, w_mem_kv, q_g, k_g):
    mk, mv = split_cols(rms_norm(mem, mem_g) @ w_mem_kv, [X_WIDTH, X_WIDTH])
    q = rms_norm(to_heads(xq, X_HEADS), q_g)
    k = rms_norm(to_heads(mk, X_HEADS), k_g)
    v = to_heads(mv, X_HEADS)
    s = jnp.einsum('bhsd,bhmd->bhsm', q, k).astype(jnp.float32) * SCALE
    p = jax.nn.softmax(s, axis=-1)
    return from_heads(jnp.einsum('bhsm,bhmd->bhsd', p.astype(v.dtype), v))


def stick_breaking_mixer(h, w_in):
    B, S, _ = h.shape
    q, k, v, xq = split_cols(h @ w_in, [D_MODEL, D_MODEL, D_MODEL, X_WIDTH])
    q, k, v = to_heads(q, N_HEADS), to_heads(k, N_HEADS), to_heads(v, N_HEADS)
    nq = S // Q_BLOCK
    qb = jnp.moveaxis(q.reshape(B, N_HEADS, nq, Q_BLOCK, HEAD_DIM), 2, 0)
    kpos = jnp.arange(S)

    def block(args):
        qi, i = args
        tpos = i * Q_BLOCK + jnp.arange(Q_BLOCK)
        z = jnp.einsum('bhqd,bhkd->bhqk', qi, k).astype(jnp.float32) * SCALE
        past = kpos[None, :] < tpos[:, None]
        log_fail = jnp.where(past, jax.nn.log_sigmoid(-z), 0.0)
        after = lax.cumsum(log_fail, axis=3, reverse=True) - log_fail
        a = jnp.where(past, jnp.exp(jax.nn.log_sigmoid(z) + after), 0.0)
        return jnp.einsum('bhqk,bhkd->bhqd', a.astype(v.dtype), v)

    o = lax.map(block, (qb, jnp.arange(nq)))
    o = o.transpose(1, 2, 0, 3, 4).reshape(B, N_HEADS, S, HEAD_DIM)
    return from_heads(o), xq


def diff_mixer(h, w_in, q_g, k_g, lam_vecs, subln_g, lam_init, rel_bias):
    B, S, _ = h.shape
    q, k, v, xq = split_cols(h @ w_in, [D_MODEL, D_MODEL, D_MODEL, X_WIDTH])
    q = rms_norm(q.reshape(B, S, DIFF_HEADS, 2, HEAD_DIM), q_g).transpose(3, 0, 2, 1, 4)
    k = rms_norm(k.reshape(B, S, DIFF_HEADS, 2, HEAD_DIM), k_g).transpose(3, 0, 2, 1, 4)
    v = to_heads(v, DIFF_HEADS)
    lv = lam_vecs.astype(jnp.float32)
    lam = jnp.exp(jnp.sum(lv[0] * lv[1])) - jnp.exp(jnp.sum(lv[2] * lv[3])) + lam_init
    nq = S // Q_BLOCK
    qb = jnp.moveaxis(q.reshape(2, B, DIFF_HEADS, nq, Q_BLOCK, HEAD_DIM), 3, 0)
    kpos = jnp.arange(S)

    def block(args):
        qi, i = args
        tpos = i * Q_BLOCK + jnp.arange(Q_BLOCK)
        dist = tpos[:, None] - kpos[None, :]
        causal = dist >= 0
        bias = rel_bias[:, jnp.clip(dist, 0, S - 1)].reshape(DIFF_HEADS, 2, Q_BLOCK, S).transpose(1, 0, 2, 3)
        s = jnp.einsum('cbhqd,cbhkd->cbhqk', qi, k).astype(jnp.float32) * SCALE + bias[:, None]
        p = masked_softmax(s, causal)
        a = p[0] - lam * p[1]
        return jnp.einsum('bhqk,bhkd->bhqd', a.astype(v.dtype), v)

    o = lax.map(block, (qb, jnp.arange(nq)))
    o = o.transpose(1, 2, 0, 3, 4).reshape(B, DIFF_HEADS, S, 2 * HEAD_DIM)
    o = rms_norm(o, subln_g) * (1.0 - lam_init)
    return from_heads(o), xq


def nsa_compress(t, pos, w1, b1, w2):
    B, G, S, dh = t.shape
    n_cmp = (S - CMP_BLOCK) // CMP_STRIDE + 1
    idx = np.arange(n_cmp)[:, None] * CMP_STRIDE + np.arange(CMP_BLOCK)[None, :]
    blocks = (t[:, :, idx] + pos).reshape(B, G, n_cmp, CMP_BLOCK * dh)
    return jax.nn.gelu(blocks @ w1 + b1) @ w2


def nsa_mixer(h, w_in, q_g, k_g, cmp_pos, cmp_w1, cmp_b1, cmp_w2, rel_bias):
    B, S, _ = h.shape
    G, R, dh = NSA_KV_GROUPS, NSA_HPG, HEAD_DIM
    kvw = G * dh
    q, kc, vc, ks, vs, kw, vw, gates, xq = split_cols(
        h @ w_in, [D_MODEL, kvw, kvw, kvw, kvw, kvw, kvw, 3 * N_HEADS, X_WIDTH])
    q = rms_norm(to_heads(q, N_HEADS), q_g)
    kc_c = rms_norm(nsa_compress(to_heads(kc, G), cmp_pos[0], cmp_w1[0], cmp_b1[0], cmp_w2[0]), k_g)
    vc_c = nsa_compress(to_heads(vc, G), cmp_pos[1], cmp_w1[1], cmp_b1[1], cmp_w2[1])
    ks = rms_norm(to_heads(ks, G), k_g)
    vs = to_heads(vs, G)
    kw = rms_norm(to_heads(kw, G), k_g)
    vw = to_heads(vw, G)
    n_cmp = kc_c.shape[2]
    n_slc = S // SLC_BLOCK
    n_sel = min(SLC_TOPN, n_slc)
    cmp_end = jnp.arange(n_cmp) * CMP_STRIDE + CMP_BLOCK - 1
    c_start = np.arange(n_cmp) * CMP_STRIDE
    s_start = np.arange(n_slc) * SLC_BLOCK
    slc_map = jnp.asarray(((c_start[:, None] < s_start[None, :] + SLC_BLOCK)
                           & (c_start[:, None] + CMP_BLOCK > s_start[None, :])).astype(np.float32))
    ks_blk = ks.reshape(B, G, n_slc, SLC_BLOCK, dh)
    vs_blk = vs.reshape(B, G, n_slc, SLC_BLOCK, dh)
    kw_pad = jnp.pad(kw, ((0, 0), (0, 0), (WINDOW, 0), (0, 0)))
    vw_pad = jnp.pad(vw, ((0, 0), (0, 0), (WINDOW, 0), (0, 0)))
    bi = jnp.arange(B)[:, None, None, None]
    gi = jnp.arange(G)[None, :, None, None]
    head_idx = (jnp.arange(G)[:, None] * R + jnp.arange(R)[None, :])[None, :, :, None, None, None]
    jb = jnp.arange(n_slc)
    nq = S // Q_BLOCK
    qb = jnp.moveaxis(q.reshape(B, G, R, nq, Q_BLOCK, dh), 3, 0)

    def block(args):
        qi, i = args
        t0 = i * Q_BLOCK
        tpos = t0 + jnp.arange(Q_BLOCK)
        s = jnp.einsum('bgrqd,bgcd->bgrqc', qi, kc_c).astype(jnp.float32) * SCALE
        p_c = masked_softmax(s, cmp_end[None, :] <= tpos[:, None])
        o_c = jnp.einsum('bgrqc,bgcd->bgrqd', p_c.astype(vc_c.dtype), vc_c)
        score = jnp.einsum('bgrqc,cj->bgqj', p_c, slc_map)
        cur = tpos // SLC_BLOCK
        forced = (jb[None, :] == 0) | (jb[None, :] == cur[:, None]) | (jb[None, :] == cur[:, None] - 1)
        valid = jb[None, :] * SLC_BLOCK <= tpos[:, None]
        score = jnp.where(forced, FORCED_SCORE, jnp.where(valid, score, -1.0))
        _, sel = lax.top_k(score, n_sel)
        k_sel = ks_blk[bi, gi, sel]
        v_sel = vs_blk[bi, gi, sel].reshape(B, G, Q_BLOCK, n_sel * SLC_BLOCK, dh)
        pos = sel[..., None] * SLC_BLOCK + jnp.arange(SLC_BLOCK)
        dist = (tpos[:, None, None] - pos)[:, :, None]
        s = (jnp.einsum('bgrqd,bgqnkd->bgrqnk', qi, k_sel).astype(jnp.float32) * SCALE
             + rel_bias[head_idx, jnp.clip(dist, 0, S - 1)])
        p_s = masked_softmax(s.reshape(B, G, R, Q_BLOCK, n_sel * SLC_BLOCK),
                             (dist >= 0).reshape(B, G, 1, Q_BLOCK, n_sel * SLC_BLOCK))
        o_s = jnp.einsum('bgrqm,bgqmd->bgrqd', p_s.astype(v_sel.dtype), v_sel)
        k_win = lax.dynamic_slice_in_dim(kw_pad, t0, WINDOW + Q_BLOCK, axis=2)
        v_win = lax.dynamic_slice_in_dim(vw_pad, t0, WINDOW + Q_BLOCK, axis=2)
        wpos = t0 - WINDOW + jnp.arange(WINDOW + Q_BLOCK)
        wd = tpos[:, None] - wpos[None, :]
        wmask = (wd >= 0) & (wd < WINDOW) & (wpos[None, :] >= 0)
        wbias = rel_bias[:, jnp.clip(wd, 0, S - 1)].reshape(G, R, Q_BLOCK, WINDOW + Q_BLOCK)
        s = jnp.einsum('bgrqd,bgkd->bgrqk', qi, k_win).astype(jnp.float32) * SCALE + wbias
        p_w = masked_softmax(s, wmask)
        o_w = jnp.einsum('bgrqk,bgkd->bgrqd', p_w.astype(v_win.dtype), v_win)
        return jnp.stack([o_c, o_s, o_w])

    o = lax.map(block, (qb, jnp.arange(nq)))
    o = jnp.moveaxis(o, 0, 4).reshape(3, B, N_HEADS, S, dh)
    g = jax.nn.sigmoid(gates.reshape(B, S, N_HEADS, 3).astype(jnp.float32)).transpose(3, 0, 2, 1)[..., None]
    o = jnp.sum(g * o.astype(jnp.float32), axis=0).astype(h.dtype)
    return from_heads(o), xq


def moba_mixer(h, w_in, q_g, k_g, rel_bias):
    B, S, _ = h.shape
    H, dh, MB, QB = N_HEADS, HEAD_DIM, MOBA_BLOCK, MOBA_QBLOCK
    q, k, v, xq = split_cols(h @ w_in, [D_MODEL, D_MODEL, D_MODEL, X_WIDTH])
    q = rms_norm(to_heads(q, H), q_g)
    k = rms_norm(to_heads(k, H), k_g)
    v = to_heads(v, H)
    nb = -(-S // MB)
    pad = nb * MB - S
    kp = jnp.pad(k, ((0, 0), (0, 0), (0, pad), (0, 0)))
    vp = jnp.pad(v, ((0, 0), (0, 0), (0, pad), (0, 0)))
    kb = kp.reshape(B, H, nb, MB, dh)
    vb = vp.reshape(B, H, nb, MB, dh)
    kmean = jnp.mean(kb.astype(jnp.float32), axis=3)
    n_top = min(MOBA_TOPK, nb - 1)
    bi = jnp.arange(B)[:, None, None, None]
    hi = jnp.arange(H)[None, :, None, None]
    hi5 = jnp.arange(H)[None, :, None, None, None]
    nq = S // QB
    qb = jnp.moveaxis(q.reshape(B, H, nq, QB, dh), 2, 0)

    def block(args):
        qi, i = args
        t0 = i * QB
        tpos = t0 + jnp.arange(QB)
        cur = t0 // MB
        own_k = lax.dynamic_slice_in_dim(kp, cur * MB, MB, axis=2)
        own_v = lax.dynamic_slice_in_dim(vp, cur * MB, MB, axis=2)
        own_d = tpos[:, None] - (cur * MB + jnp.arange(MB))[None, :]
        s_own = (jnp.einsum('bhqd,bhkd->bhqk', qi, own_k).astype(jnp.float32) * SCALE
                 + rel_bias[:, jnp.clip(own_d, 0, S - 1)])
        m_own = jnp.broadcast_to(own_d >= 0, (B, H, QB, MB))
        if n_top == 0:
            p_own = masked_softmax(s_own, m_own)
            return jnp.einsum('bhqk,bhkd->bhqd', p_own.astype(v.dtype), own_v)
        gate = jnp.einsum('bhqd,bhjd->bhqj', qi.astype(jnp.float32), kmean)
        gate = jnp.where(jnp.arange(nb) < cur, gate, NEG_INF)
        _, sel = lax.top_k(gate, n_top)
        sel_ok = sel < cur
        k_sel = kb[bi, hi, sel]
        v_sel = vb[bi, hi, sel].reshape(B, H, QB, n_top * MB, dh)
        dist = tpos[:, None, None] - (sel[..., None] * MB + jnp.arange(MB))
        s_sel = (jnp.einsum('bhqd,bhqnkd->bhqnk', qi, k_sel).astype(jnp.float32) * SCALE
                 + rel_bias[hi5, jnp.clip(dist, 0, S - 1)])
        s = jnp.concatenate([s_sel.reshape(B, H, QB, n_top * MB), s_own], axis=-1)
        m_sel = jnp.broadcast_to(sel_ok[..., None], (B, H, QB, n_top, MB)).reshape(B, H, QB, n_top * MB)
        p = masked_softmax(s, jnp.concatenate([m_sel, m_own], axis=-1))
        p_sel, p_own = p[..., :n_top * MB], p[..., n_top * MB:]
        return (jnp.einsum('bhqm,bhqmd->bhqd', p_sel.astype(v.dtype), v_sel)
                + jnp.einsum('bhqk,bhkd->bhqd', p_own.astype(v.dtype), own_v))

    o = lax.map(block, (qb, jnp.arange(nq)))
    o = o.transpose(1, 2, 0, 3, 4).reshape(B, H, S, dh)
    return from_heads(o), xq


def setup_inputs(seed: int = 0) -> dict:
    key = jax.random.key(seed)
    keys = iter(jax.random.split(key, 40))

    def nrm(shape, scale):
        return scale * jax.random.normal(next(keys), shape, jnp.float32)

    def gain(shape):
        return 1.0 + 0.02 * jax.random.normal(next(keys), shape, jnp.float32)

    nA, nB, nC, nD = (len(range(t, DEPTH, N_MIXERS)) for t in range(N_MIXERS))
    D = D_MODEL
    return {
        "x": nrm((BATCH, SEQ, D), 1.0),
        "mem": nrm((BATCH, MEM_LEN, D), 1.0),
        "rel_table": nrm((NUM_BUCKETS, N_HEADS), 0.2),
        "attn_norm": gain((DEPTH, D)),
        "mem_norm": gain((DEPTH, D)),
        "w_mem_kv": nrm((DEPTH, D, 2 * X_WIDTH), D ** -0.5),
        "xq_norm": gain((DEPTH, HEAD_DIM)),
        "xk_norm": gain((DEPTH, HEAD_DIM)),
        "w_out": nrm((DEPTH, MIX_OUT, D), MIX_OUT ** -0.5),
        "ffn_norm": gain((DEPTH, D)),
        "w_gate": nrm((DEPTH, D, D_FF), D ** -0.5),
        "w_up": nrm((DEPTH, D, D_FF), D ** -0.5),
        "w_down": nrm((DEPTH, D_FF, D), D_FF ** -0.5),
        "sb_w_in": nrm((nA, D, SB_IN), D ** -0.5),
        "diff_w_in": nrm((nB, D, DIFF_IN), D ** -0.5),
        "diff_q_norm": gain((nB, HEAD_DIM)),
        "diff_k_norm": gain((nB, HEAD_DIM)),
        "diff_lambda": nrm((nB, 4, HEAD_DIM), 0.1),
        "diff_subln": gain((nB, 2 * HEAD_DIM)),
        "nsa_w_in": nrm((nC, D, NSA_IN), D ** -0.5),
        "nsa_q_norm": gain((nC, HEAD_DIM)),
        "nsa_k_norm": gain((nC, HEAD_DIM)),
        "nsa_cmp_pos": nrm((nC, 2, CMP_BLOCK, HEAD_DIM), 0.1),
        "nsa_cmp_w1": nrm((nC, 2, CMP_BLOCK * HEAD_DIM, CMP_HIDDEN), (CMP_BLOCK * HEAD_DIM) ** -0.5),
        "nsa_cmp_b1": nrm((nC, 2, CMP_HIDDEN), 0.01),
        "nsa_cmp_w2": nrm((nC, 2, CMP_HIDDEN, HEAD_DIM), CMP_HIDDEN ** -0.5),
        "moba_w_in": nrm((nD, D, MOBA_IN), D ** -0.5),
        "moba_q_norm": gain((nD, HEAD_DIM)),
        "moba_k_norm": gain((nD, HEAD_DIM)),
    }


def reference(x, mem, rel_table, attn_norm, mem_norm, w_mem_kv, xq_norm, xk_norm, w_out,
              ffn_norm, w_gate, w_up, w_down, sb_w_in, diff_w_in, diff_q_norm, diff_k_norm,
              diff_lambda, diff_subln, nsa_w_in, nsa_q_norm, nsa_k_norm, nsa_cmp_pos,
              nsa_cmp_w1, nsa_cmp_b1, nsa_cmp_w2, moba_w_in, moba_q_norm, moba_k_norm):
    S = x.shape[1]
    rel_bias = rel_table.astype(jnp.float32)[t5_bucket(jnp.arange(S))].T
    for i in range(DEPTH):
        kind, j = i % N_MIXERS, i // N_MIXERS
        hn = rms_norm(x, attn_norm[i])
        if kind == 0:
            mix, xq = stick_breaking_mixer(hn, sb_w_in[j])
        elif kind == 1:
            lam_init = 0.8 - 0.6 * math.exp(-0.3 * i)
            mix, xq = diff_mixer(hn, diff_w_in[j], diff_q_norm[j], diff_k_norm[j], diff_lambda[j],
                                 diff_subln[j], lam_init, rel_bias)
        elif kind == 2:
            mix, xq = nsa_mixer(hn, nsa_w_in[j], nsa_q_norm[j], nsa_k_norm[j], nsa_cmp_pos[j],
                                nsa_cmp_w1[j], nsa_cmp_b1[j], nsa_cmp_w2[j], rel_bias)
        else:
            mix, xq = moba_mixer(hn, moba_w_in[j], moba_q_norm[j], moba_k_norm[j], rel_bias)
        cross = cross_attention(xq, mem, mem_norm[i], w_mem_kv[i], xq_norm[i], xk_norm[i])
        x = x + jnp.concatenate([mix, cross], axis=-1) @ w_out[i]
        x = x + swiglu(rms_norm(x, ffn_norm[i]), w_gate[i], w_up[i], w_down[i])
    return x
```

```python
import functools
import math

import numpy as np
import jax
import jax.numpy as jnp
from jax import lax
from jax.experimental import pallas as pl
from jax.experimental.pallas import tpu as pltpu

F32 = jnp.float32
BF16 = jnp.bfloat16

D_MODEL = 1024
HEAD_DIM = 64
N_HEADS = 16
LANES = 128
X_HEADS = 4
X_WIDTH = X_HEADS * HEAD_DIM
D_FF = 2816
NUM_BUCKETS = 32
MAX_DISTANCE = 128
EPS = 1e-6
NEG_INF = -1e30
SCALE = HEAD_DIM ** -0.5
NSA_GROUPS = 4
NSA_HPG = 4
CMP_BLOCK = 32
CMP_STRIDE = 16
CMP_HIDDEN = 256
SLC_BLOCK = 64
SLC_TOPN = 16
WINDOW = 512
FORCED_SCORE = 1e4
MOBA_BLOCK = 256
MOBA_TOPK = 3
F32_EXP_UNDERFLOW = -104.0
VMEM_LIMIT_BYTES = 56 * 1024 * 1024


def _params(*sem):
    return pltpu.CompilerParams(dimension_semantics=sem, vmem_limit_bytes=VMEM_LIMIT_BYTES)


def _dot(a, b):
    return jnp.dot(a, b, preferred_element_type=F32)


def _dot_nt(a, b):
    return lax.dot_general(a, b, (((1,), (1,)), ((), ())), preferred_element_type=F32)


def _split2(x):
    hi = x.astype(BF16)
    lo = (x - hi.astype(F32)).astype(BF16)
    return hi, lo


def _split3(x):
    hi = x.astype(BF16)
    r = x - hi.astype(F32)
    mid = r.astype(BF16)
    lo = (r - mid.astype(F32)).astype(BF16)
    return hi, mid, lo


def _dot_exact_rhs(x, m_bf16, parts=2):
    ps = _split2(x) if parts == 2 else _split3(x)
    out = _dot(ps[0], m_bf16)
    for p in ps[1:]:
        out = out + _dot(p, m_bf16)
    return out


def _dot3(a, b, nt=False):
    f = _dot_nt if nt else _dot
    ah, al = _split2(a)
    bh, bl = _split2(b)
    return f(ah, bh) + (f(ah, bl) + f(al, bh))


def _half_is(shape, e):
    lane = lax.broadcasted_iota(jnp.int32, shape, len(shape) - 1)
    return (lane // HEAD_DIM) == e


def _softmax_step(s, mask, v_bf16, m_sc, l_sc, acc_sc):
    if mask is not None:
        s = jnp.where(mask, s, NEG_INF)
    m_old = m_sc[...]
    m_new = jnp.maximum(m_old, jnp.max(s, axis=1, keepdims=True))
    p = jnp.exp(s - m_new)
    if mask is not None:
        p = jnp.where(mask, p, 0.0)
    alpha = jnp.exp(m_old - m_new)
    l_sc[...] = alpha * l_sc[...] + jnp.sum(p, axis=1, keepdims=True)
    acc_sc[...] = alpha * acc_sc[...] + _dot(p.astype(BF16), v_bf16)
    m_sc[...] = m_new


def _softmax_init(m_sc, l_sc, acc_sc):
    m_sc[...] = jnp.full(m_sc.shape, NEG_INF, F32)
    l_sc[...] = jnp.zeros(l_sc.shape, F32)
    acc_sc[...] = jnp.zeros(acc_sc.shape, F32)


def _softmax_result(l_sc, acc_sc):
    return acc_sc[...] / jnp.maximum(l_sc[...], 1e-30)


def _proj_kernel(x_ref, g_ref, w_ref, hg_ref, o_ref, *, tn, head_norm):
    x = x_ref[...]
    hn = x * lax.rsqrt(jnp.mean(x * x, axis=-1, keepdims=True) + EPS) * g_ref[...]
    hb = hn.astype(BF16)
    n = o_ref.shape[-1]
    if head_norm:
        r = lax.broadcasted_iota(jnp.int32, (tn, tn), 0) // HEAD_DIM
        c = lax.broadcasted_iota(jnp.int32, (tn, tn), 1) // HEAD_DIM
        same_head = jnp.where(r == c, 1.0, 0.0).astype(BF16)
    for cb in range(n // tn):
        cols = slice(cb * tn, (cb + 1) * tn)
        y = _dot(hb, w_ref[:, cols])
        if head_norm:
            ms = _dot_exact_rhs(y * y, same_head) * (1.0 / HEAD_DIM)
            y = y * lax.rsqrt(ms + EPS) * hg_ref[:, cols]
        o_ref[:, cols] = y.astype(o_ref.dtype)


def _project(x2d, gain, w_bf16, head_gain=None, tm=256):
    m, d = x2d.shape
    n = w_bf16.shape[1]
    tn = 256 if n % 256 == 0 else LANES
    head_norm = head_gain is not None
    if head_gain is None:
        head_gain = jnp.ones((1, n), F32)
    return pl.pallas_call(
        functools.partial(_proj_kernel, tn=tn, head_norm=head_norm),
        grid=(m // tm,),
        in_specs=[
            pl.BlockSpec((tm, d), lambda i: (i, 0)),
            pl.BlockSpec((1, d), lambda i: (0, 0)),
            pl.BlockSpec((d, n), lambda i: (0, 0)),
            pl.BlockSpec((1, n), lambda i: (0, 0)),
        ],
        out_specs=pl.BlockSpec((tm, n), lambda i: (i, 0)),
        out_shape=jax.ShapeDtypeStruct((m, n), F32),
        compiler_params=_params("parallel"),
        name="proj_norm" if head_norm else "proj_plain",
    )(x2d, gain.reshape(1, d).astype(F32), w_bf16, head_gain.reshape(1, n).astype(F32))


def _outproj_kernel(x_ref, mix_ref, cross_ref, w1_ref, w2_ref, o_ref):
    y = _dot(mix_ref[...].astype(BF16), w1_ref[...])
    y = y + _dot(cross_ref[...].astype(BF16), w2_ref[...])
    o_ref[...] = x_ref[...] + y


def _out_project(x2d, mix2d, cross2d, w_out, tm=512):
    m, d = x2d.shape
    w1 = w_out[:D_MODEL].astype(BF16)
    w2 = w_out[D_MODEL:].astype(BF16)
    return pl.pallas_call(
        _outproj_kernel,
        grid=(m // tm,),
        in_specs=[
            pl.BlockSpec((tm, d), lambda i: (i, 0)),
            pl.BlockSpec((tm, D_MODEL), lambda i: (i, 0)),
            pl.BlockSpec((tm, X_WIDTH), lambda i: (i, 0)),
            pl.BlockSpec((D_MODEL, d), lambda i: (0, 0)),
            pl.BlockSpec((X_WIDTH, d), lambda i: (0, 0)),
        ],
        out_specs=pl.BlockSpec((tm, d), lambda i: (i, 0)),
        out_shape=jax.ShapeDtypeStruct((m, d), F32),
        compiler_params=_params("parallel"),
        name="out_proj",
    )(x2d, mix2d, cross2d, w1, w2)


def _ffn_kernel(x_ref, g_ref, wg_ref, wu_ref, wd_ref, o_ref, hn_sc, acc_sc):
    f = pl.program_id(1)

    @pl.when(f == 0)
    def _():
        x = x_ref[...]
        hn = x * lax.rsqrt(jnp.mean(x * x, axis=-1, keepdims=True) + EPS) * g_ref[...]
        hn_sc[...] = hn.astype(BF16)
        acc_sc[...] = x

    hb = hn_sc[...]
    gate = _dot(hb, wg_ref[...])
    up = _dot(hb, wu_ref[...])
    act = gate * (1.0 / (1.0 + jnp.exp(-gate))) * up
    acc_sc[...] += _dot(act.astype(BF16), wd_ref[...])

    @pl.when(f == pl.num_programs(1) - 1)
    def _():
        o_ref[...] = acc_sc[...]


def _ffn(x2d, gain, w_gate, w_up, w_down, tm=1024, tf=256):
    m, d = x2d.shape
    dff = w_gate.shape[1]
    return pl.pallas_call(
        _ffn_kernel,
        grid=(m // tm, dff // tf),
        in_specs=[
            pl.BlockSpec((tm, d), lambda i, f: (i, 0)),
            pl.BlockSpec((1, d), lambda i, f: (0, 0)),
            pl.BlockSpec((d, tf), lambda i, f: (0, f)),
            pl.BlockSpec((d, tf), lambda i, f: (0, f)),
            pl.BlockSpec((tf, d), lambda i, f: (f, 0)),
        ],
        out_specs=pl.BlockSpec((tm, d), lambda i, f: (i, 0)),
        out_shape=jax.ShapeDtypeStruct((m, d), F32),
        scratch_shapes=[pltpu.VMEM((tm, d), BF16), pltpu.VMEM((tm, d), F32)],
        compiler_params=_params("parallel", "arbitrary"),
        name="ffn",
    )(x2d, gain.reshape(1, d).astype(F32), w_gate.astype(BF16), w_up.astype(BF16), w_down.astype(BF16))


def _cross_kernel(q_ref, k_ref, v_ref, o_ref):
    tq = q_ref.shape[1]
    for pb in range(X_WIDTH // LANES):
        cols = slice(pb * LANES, (pb + 1) * LANES)
        q = q_ref[0, :, cols]
        k = k_ref[0, :, cols].astype(BF16)
        v = v_ref[0, :, cols].astype(BF16)
        outs = []
        for e in (0, 1):
            qm = jnp.where(_half_is((tq, LANES), e), q, 0.0).astype(BF16)
            s = _dot_nt(qm, k)
            m = jnp.max(s, axis=1, keepdims=True)
            p = jnp.exp(s - m)
            outs.append(_dot(p.astype(BF16), v) / jnp.sum(p, axis=1, keepdims=True))
        o_ref[0, :, cols] = jnp.where(_half_is((tq, LANES), 0), outs[0], outs[1])


def _cross_attention(xq, mk, mv, tq=512):
    b, s, _ = xq.shape
    mlen = mk.shape[1]
    return pl.pallas_call(
        _cross_kernel,
        grid=(b, s // tq),
        in_specs=[
            pl.BlockSpec((1, tq, X_WIDTH), lambda bi, i: (bi, i, 0)),
            pl.BlockSpec((1, mlen, X_WIDTH), lambda bi, i: (bi, 0, 0)),
            pl.BlockSpec((1, mlen, X_WIDTH), lambda bi, i: (bi, 0, 0)),
        ],
        out_specs=pl.BlockSpec((1, tq, X_WIDTH), lambda bi, i: (bi, i, 0)),
        out_shape=jax.ShapeDtypeStruct((b, s, X_WIDTH), F32),
        compiler_params=_params("parallel", "parallel"),
        name="cross_attn",
    )(xq, mk, mv)


def _t5_bucket(dist):
    max_exact = NUM_BUCKETS // 2
    large = max_exact + (jnp.log(jnp.maximum(dist, 1).astype(F32) / max_exact)
                         / math.log(MAX_DISTANCE / max_exact) * (NUM_BUCKETS - max_exact)).astype(jnp.int32)
    large = jnp.minimum(large, NUM_BUCKETS - 1)
    return jnp.where(dist < max_exact, dist, large)


def _bias_tiles(rel_bias, t):
    assert t >= MAX_DISTANCE
    s = rel_bias.shape[1]
    r = np.arange(t)[:, None]
    c = np.arange(t)[None, :]
    dist = np.stack([r - c, t + r - c])
    tiles = rel_bias[:, np.clip(dist, 0, s - 1)]
    far = jnp.broadcast_to(rel_bias[:, 2 * t][:, None, None], (rel_bias.shape[0], 1, t))
    return tiles.astype(F32), far.astype(F32)


def _sb_kernel(q_ref, k_ref, v_ref, o_ref, *, t):
    qi = pl.program_id(2)
    q = q_ref[0]
    row = lax.broadcasted_iota(jnp.int32, (t, t), 0)
    col = lax.broadcasted_iota(jnp.int32, (t, t), 1)
    later = jnp.where(row > col, 1.0, 0.0).astype(BF16)
    past = col < row

    def tile(qm, j, carry, acc, diag):
        start = pl.multiple_of(j * t, t)
        k = k_ref[0, pl.ds(start, t), :].astype(BF16)
        v = v_ref[0, pl.ds(start, t), :].astype(BF16)
        z = _dot_nt(qm, k)
        soft = jnp.log1p(jnp.exp(-jnp.abs(z)))
        log_beta = jnp.minimum(z, 0.0) - soft
        log_fail = -jnp.maximum(z, 0.0) - soft
        if diag:
            log_fail = jnp.where(past, log_fail, 0.0)
        within = _dot_exact_rhs(log_fail, later)
        a = jnp.exp(log_beta + within + carry)
        if diag:
            a = jnp.where(past, a, 0.0)
        acc = acc + _dot(a.astype(BF16), v)
        carry = carry + jnp.sum(log_fail, axis=1, keepdims=True)
        return carry, acc

    outs = []
    for e in (0, 1):
        qm = jnp.where(_half_is((t, LANES), e), q, 0.0).astype(BF16)
        carry, acc = tile(qm, qi, jnp.zeros((t, 1), F32), jnp.zeros((t, LANES), F32), True)

        def cond(st):
            return jnp.logical_and(st[0] >= 0, jnp.max(st[1]) > F32_EXP_UNDERFLOW)

        def body(st, qm=qm):
            carry, acc = tile(qm, st[0], st[1], st[2], False)
            return st[0] - 1, carry, acc

        _, carry, acc = lax.while_loop(cond, body, (qi - 1, carry, acc))
        outs.append(acc)
    o_ref[0] = jnp.where(_half_is((t, LANES), 0), outs[0], outs[1])


def _sb_attention(qkv, t=128):
    b, s, _ = qkv.shape
    nb = D_MODEL // LANES
    return pl.pallas_call(
        functools.partial(_sb_kernel, t=t),
        grid=(b, nb, s // t),
        in_specs=[
            pl.BlockSpec((1, t, LANES), lambda bi, h, i: (bi, i, h)),
            pl.BlockSpec((1, s, LANES), lambda bi, h, i: (bi, 0, nb + h)),
            pl.BlockSpec((1, s, LANES), lambda bi, h, i: (bi, 0, 2 * nb + h)),
        ],
        out_specs=pl.BlockSpec((1, t, LANES), lambda bi, h, i: (bi, i, h)),
        out_shape=jax.ShapeDtypeStruct((b, s, D_MODEL), F32),
        compiler_params=_params("parallel", "parallel", "arbitrary"),
        name="sb_attn",
    )(qkv, qkv, qkv)


def _diff_kernel(lam_ref, q_ref, k_ref, v_ref, bias_ref, far_ref, sg_ref, o_ref,
                 m_sc, l_sc, acc_sc, *, t, post_scale):
    qi = pl.program_id(2)
    q = q_ref[0]
    row = lax.broadcasted_iota(jnp.int32, (t, t), 0)
    col = lax.broadcasted_iota(jnp.int32, (t, t), 1)
    causal = col <= row

    def step(qm, j, bias, mask):
        start = pl.multiple_of(j * t, t)
        k = k_ref[0, pl.ds(start, t), :].astype(BF16)
        v = v_ref[0, pl.ds(start, t), :].astype(BF16)
        _softmax_step(_dot_nt(qm, k) + bias, mask, v, m_sc, l_sc, acc_sc)

    res = []
    for c in (0, 1):
        qm = jnp.where(_half_is((t, LANES), c), q, 0.0).astype(BF16)
        _softmax_init(m_sc, l_sc, acc_sc)

        def far_body(j, carry, qm=qm, c=c):
            step(qm, j, far_ref[c], None)
            return carry

        lax.fori_loop(0, jnp.maximum(qi - 1, 0), far_body, 0)

        @pl.when(qi >= 1)
        def _(qm=qm, c=c):
            step(qm, qi - 1, bias_ref[c, 1], None)

        step(qm, qi, bias_ref[c, 0], causal)
        res.append(_softmax_result(l_sc, acc_sc))
    o = res[0] - lam_ref[0] * res[1]
    o = o * lax.rsqrt(jnp.mean(o * o, axis=-1, keepdims=True) + EPS) * sg_ref[...] * post_scale
    o_ref[0] = o


def _diff_attention(qk, v, lam, subln_g, rel_bias, post_scale, t=256):
    b, s, _ = qk.shape
    nb = D_MODEL // LANES
    tiles, far = _bias_tiles(rel_bias, t)
    return pl.pallas_call(
        functools.partial(_diff_kernel, t=t, post_scale=post_scale),
        grid=(b, nb, s // t),
        in_specs=[
            pl.BlockSpec(memory_space=pltpu.SMEM),
            pl.BlockSpec((1, t, LANES), lambda bi, h, i: (bi, i, h)),
            pl.BlockSpec((1, s, LANES), lambda bi, h, i: (bi, 0, nb + h)),
            pl.BlockSpec((1, s, LANES), lambda bi, h, i: (bi, 0, h)),
            pl.BlockSpec((2, 2, t, t), lambda bi, h, i: (h, 0, 0, 0)),
            pl.BlockSpec((2, 1, t), lambda bi, h, i: (h, 0, 0)),
            pl.BlockSpec((1, LANES), lambda bi, h, i: (0, 0)),
        ],
        out_specs=pl.BlockSpec((1, t, LANES), lambda bi, h, i: (bi, i, h)),
        out_shape=jax.ShapeDtypeStruct((b, s, D_MODEL), F32),
        scratch_shapes=[pltpu.VMEM((t, 1), F32), pltpu.VMEM((t, 1), F32), pltpu.VMEM((t, LANES), F32)],
        compiler_params=_params("parallel", "parallel", "arbitrary"),
        name="diff_attn",
    )(lam.reshape(1).astype(F32), qk, qk, v, tiles, far, subln_g.reshape(1, LANES).astype(F32))


def _moba_kernel(q_ref, k_ref, v_ref, bias_ref, far_ref, o_ref, kmean_sc, m_sc, l_sc, acc_sc, *, t, nb):
    qi = pl.program_id(2)

    @pl.when(qi == 0)
    def _():
        for j in range(nb):
            kmean_sc[j:j + 1, :] = jnp.mean(k_ref[0, j * t:(j + 1) * t, :], axis=0, keepdims=True)

    q = q_ref[0]
    row = lax.broadcasted_iota(jnp.int32, (t, t), 0)
    col = lax.broadcasted_iota(jnp.int32, (t, t), 1)
    causal = col <= row
    blk = lax.broadcasted_iota(jnp.int32, (t, nb), 1)
    blk_f = blk.astype(F32)
    kmean = kmean_sc[...]

    def step(qm, j, bias, mask):
        start = pl.multiple_of(j * t, t)
        k = k_ref[0, pl.ds(start, t), :].astype(BF16)
        v = v_ref[0, pl.ds(start, t), :].astype(BF16)
        _softmax_step(_dot_nt(qm, k) + bias, mask, v, m_sc, l_sc, acc_sc)

    outs = []
    for e in (0, 1):
        qh = jnp.where(_half_is((t, LANES), e), q, 0.0)
        qm = qh.astype(BF16)
        gate = jnp.where(blk < qi, _dot3(qh, kmean, nt=True), NEG_INF)
        picked = jnp.zeros((t, nb), F32)
        for _ in range(MOBA_TOPK):
            top = jnp.max(gate, axis=1, keepdims=True)
            first = jnp.min(jnp.where(gate == top, blk_f, float(nb)), axis=1, keepdims=True)
            hit = blk_f == first
            picked = jnp.where(hit, 1.0, picked)
            gate = jnp.where(hit, -jnp.inf, gate)
        picked = jnp.where(blk < qi, picked, 0.0)
        _softmax_init(m_sc, l_sc, acc_sc)

        def picked_rows(j, picked=picked):
            return jnp.sum(jnp.where(blk == j, picked, 0.0), axis=1, keepdims=True) > 0.5

        def far_body(j, carry, qm=qm, e=e):
            step(qm, j, far_ref[e], picked_rows(j))
            return carry

        lax.fori_loop(0, jnp.maximum(qi - 1, 0), far_body, 0)

        @pl.when(qi >= 1)
        def _(qm=qm, e=e):
            step(qm, qi - 1, bias_ref[e, 1], picked_rows(qi - 1))

        step(qm, qi, bias_ref[e, 0], causal)
        outs.append(_softmax_result(l_sc, acc_sc))
    o_ref[0] = jnp.where(_half_is((t, LANES), 0), outs[0], outs[1])


def _moba_attention(qk, v, rel_bias):
    b, s, _ = qk.shape
    t = MOBA_BLOCK
    nb = s // t
    assert s % t == 0 and nb - 1 >= MOBA_TOPK
    nbl = D_MODEL // LANES
    tiles, far = _bias_tiles(rel_bias, t)
    return pl.pallas_call(
        functools.partial(_moba_kernel, t=t, nb=nb),
        grid=(b, nbl, nb),
        in_specs=[
            pl.BlockSpec((1, t, LANES), lambda bi, h, i: (bi, i, h)),
            pl.BlockSpec((1, s, LANES), lambda bi, h, i: (bi, 0, nbl + h)),
            pl.BlockSpec((1, s, LANES), lambda bi, h, i: (bi, 0, h)),
            pl.BlockSpec((2, 2, t, t), lambda bi, h, i: (h, 0, 0, 0)),
            pl.BlockSpec((2, 1, t), lambda bi, h, i: (h, 0, 0)),
        ],
        out_specs=pl.BlockSpec((1, t, LANES), lambda bi, h, i: (bi, i, h)),
        out_shape=jax.ShapeDtypeStruct((b, s, D_MODEL), F32),
        scratch_shapes=[pltpu.VMEM((nb, LANES), F32), pltpu.VMEM((t, 1), F32),
                        pltpu.VMEM((t, 1), F32), pltpu.VMEM((t, LANES), F32)],
        compiler_params=_params("parallel", "parallel", "arbitrary"),
        name="moba_attn",
    )(qk, qk, v, tiles, far)


NSA_TQ = 128


def _gelu_tanh(x):
    return 0.5 * x * (1.0 + jnp.tanh(math.sqrt(2.0 / math.pi) * (x + 0.044715 * (x * x * x))))


def _nsa_compress_kernel(t_ref, pos_ref, w1_ref, b1_ref, w2_ref, kg_ref, o_ref):
    kv = pl.program_id(2)
    tok = t_ref[0, 0, 0]
    n16, half = tok.shape
    first = _dot3(tok, w1_ref[0, :half, :])
    second = _dot3(tok, w1_ref[0, half:, :])
    const = _dot3(pos_ref[0], w1_ref[0]) + b1_ref[0]
    pre = first + pltpu.roll(second, n16 - 1, axis=0) + const
    y = _dot3(_gelu_tanh(pre), w2_ref[0, 0])
    normed = y * lax.rsqrt(jnp.sum(y * y, axis=-1, keepdims=True) * (1.0 / HEAD_DIM) + EPS) * kg_ref[0]
    o_ref[0, 0, 0] = jnp.where(kv == 0, normed, y)


def _nsa_compress(kc, vc, cmp_pos, cmp_w1, cmp_b1, cmp_w2, k_g):
    b, s, _ = kc.shape
    g = NSA_GROUPS
    n16 = s // CMP_STRIDE
    assert CMP_BLOCK == 2 * CMP_STRIDE
    tok = jnp.stack([kc, vc]).reshape(2, b, n16, CMP_STRIDE, g, HEAD_DIM)
    tok = tok.transpose(0, 1, 4, 2, 3, 5).reshape(2, b, g, n16, CMP_STRIDE * HEAD_DIM)
    pos = cmp_pos.reshape(2, 1, CMP_BLOCK * HEAD_DIM).astype(F32)
    zeros = jnp.zeros_like(cmp_w2)
    w2 = jnp.stack([jnp.concatenate([cmp_w2, zeros], axis=-1),
                    jnp.concatenate([zeros, cmp_w2], axis=-1)], axis=1).astype(F32)
    kg = jnp.stack([jnp.concatenate([k_g, jnp.zeros_like(k_g)]),
                    jnp.concatenate([jnp.zeros_like(k_g), k_g])]).reshape(2, 1, LANES).astype(F32)
    hid = cmp_w1.shape[-1]
    return pl.pallas_call(
        _nsa_compress_kernel,
        grid=(b, g, 2),
        in_specs=[
            pl.BlockSpec((1, 1, 1, n16, CMP_STRIDE * HEAD_DIM), lambda bi, gi, kv: (kv, bi, gi, 0, 0)),
            pl.BlockSpec((1, 1, CMP_BLOCK * HEAD_DIM), lambda bi, gi, kv: (kv, 0, 0)),
            pl.BlockSpec((1, CMP_BLOCK * HEAD_DIM, hid), lambda bi, gi, kv: (kv, 0, 0)),
            pl.BlockSpec((1, 1, hid), lambda bi, gi, kv: (kv, 0, 0)),
            pl.BlockSpec((1, 1, hid, LANES), lambda bi, gi, kv: (kv, gi % 2, 0, 0)),
            pl.BlockSpec((1, 1, LANES), lambda bi, gi, kv: (gi % 2, 0, 0)),
        ],
        out_specs=pl.BlockSpec((1, 1, 1, n16, LANES), lambda bi, gi, kv: (kv, bi, gi, 0, 0)),
        out_shape=jax.ShapeDtypeStruct((2, b, g, n16, LANES), F32),
        compiler_params=_params("parallel", "parallel", "parallel"),
        name="nsa_compress",
    )(tok, pos, cmp_w1.astype(F32), cmp_b1.reshape(2, 1, hid).astype(F32), w2, kg)


def _stack_group_heads(q_ref, e):
    tq = q_ref.shape[1]
    in_half = _half_is((tq, LANES), e)
    parts = []
    for r in range(NSA_HPG):
        x = q_ref[0, :, (r // 2) * LANES:(r // 2 + 1) * LANES]
        moved = jnp.where(e == r % 2, x, pltpu.roll(x, HEAD_DIM, axis=1))
        parts.append(jnp.where(in_half, moved, 0.0))
    return jnp.concatenate(parts, axis=0)


def _unstack_group_heads(o, e, o_ref):
    tq = o_ref.shape[1]
    lower = _half_is((tq, LANES), 0)
    heads = []
    for r in range(NSA_HPG):
        x = o[r * tq:(r + 1) * tq]
        heads.append(jnp.where(e == r % 2, x, pltpu.roll(x, HEAD_DIM, axis=1)))
    for pb in range(NSA_HPG // 2):
        o_ref[0, :, pb * LANES:(pb + 1) * LANES] = jnp.where(lower, heads[2 * pb], heads[2 * pb + 1])


def _nsa_cmp_kernel(q_ref, kc_ref, vc_ref, map_ref, oc_ref, sel_ref, *, n_slc):
    g = pl.program_id(1)
    qi = pl.program_id(2)
    e = g % 2
    tq = q_ref.shape[1]
    n16 = kc_ref.shape[3]
    q4 = _stack_group_heads(q_ref, e)
    s = _dot3(q4, kc_ref[0, 0, 0], nt=True)
    tpos4 = qi * tq + lax.broadcasted_iota(jnp.int32, (NSA_HPG * tq, n16), 0) % tq
    cmp_end = lax.broadcasted_iota(jnp.int32, (NSA_HPG * tq, n16), 1) * CMP_STRIDE + (CMP_BLOCK - 1)
    mask = cmp_end <= tpos4
    s = jnp.where(mask, s, NEG_INF)
    p = jnp.where(mask, jnp.exp(s - jnp.max(s, axis=1, keepdims=True)), 0.0)
    p = p / jnp.maximum(jnp.sum(p, axis=1, keepdims=True), 1e-30)
    o_c = _dot(p.astype(BF16), vc_ref[0, 0, 0].astype(BF16))
    _unstack_group_heads(o_c, e, oc_ref)

    p_sum = p[0:tq] + p[tq:2 * tq] + p[2 * tq:3 * tq] + p[3 * tq:4 * tq]
    score = _dot_exact_rhs(p_sum, map_ref[...], parts=3)
    width = score.shape[1]
    blk = lax.broadcasted_iota(jnp.int32, (tq, width), 1)
    blk_f = blk.astype(F32)
    tpos = qi * tq + lax.broadcasted_iota(jnp.int32, (tq, width), 0)
    cur = tpos // SLC_BLOCK
    forced = (blk == 0) | (blk == cur) | (blk == cur - 1)
    valid = blk * SLC_BLOCK <= tpos
    score = jnp.where(forced, FORCED_SCORE, jnp.where(valid, score, -1.0))
    score = jnp.where(blk < n_slc, score, -jnp.inf)
    picked = jnp.zeros((tq, width), F32)
    for _ in range(min(SLC_TOPN, n_slc)):
        top = jnp.max(score, axis=1, keepdims=True)
        first = jnp.min(jnp.where(score == top, blk_f, float(width)), axis=1, keepdims=True)
        hit = blk_f == first
        picked = jnp.where(hit, 1.0, picked)
        score = jnp.where(hit, -jnp.inf, score)
    sel_ref[0, 0] = picked


def _nsa_sel_kernel(q_ref, sel_ref, k_ref, v_ref, bias_ref, far_ref, o_ref, m_sc, l_sc, acc_sc):
    g = pl.program_id(1)
    qi = pl.program_id(2)
    e = g % 2
    t = q_ref.shape[1]
    rows = NSA_HPG * t
    q4 = _stack_group_heads(q_ref, e).astype(BF16)
    sel = sel_ref[0, 0].astype(BF16)
    width = sel.shape[1]
    per_tile = t // SLC_BLOCK
    blk_row = lax.broadcasted_iota(jnp.int32, (width, t), 0)
    key_blk = lax.broadcasted_iota(jnp.int32, (width, t), 1) // SLC_BLOCK
    causal = (lax.broadcasted_iota(jnp.int32, (rows, t), 1)
              <= lax.broadcasted_iota(jnp.int32, (rows, t), 0) % t)
    bias_diag = jnp.concatenate([bias_ref[r, 0] for r in range(NSA_HPG)], axis=0)
    bias_near = jnp.concatenate([bias_ref[r, 1] for r in range(NSA_HPG)], axis=0)
    bias_far = jnp.concatenate([jnp.broadcast_to(far_ref[r], (t, t)) for r in range(NSA_HPG)], axis=0)
    _softmax_init(m_sc, l_sc, acc_sc)

    def step(j, bias, extra_mask):
        start = pl.multiple_of(j * t, t)
        k = k_ref[0, pl.ds(start, t), :].astype(BF16)
        v = v_ref[0, pl.ds(start, t), :].astype(BF16)
        expand = jnp.where(blk_row == j * per_tile + key_blk, 1.0, 0.0).astype(BF16)
        chosen = _dot(sel, expand) > 0.5
        mask = jnp.concatenate([chosen] * NSA_HPG, axis=0)
        if extra_mask is not None:
            mask = jnp.logical_and(mask, extra_mask)
        _softmax_step(_dot_nt(q4, k) + bias, mask, v, m_sc, l_sc, acc_sc)

    def far_body(j, carry):
        step(j, bias_far, None)
        return carry

    lax.fori_loop(0, jnp.maximum(qi - 1, 0), far_body, 0)

    @pl.when(qi >= 1)
    def _():
        step(qi - 1, bias_near, None)

    step(qi, bias_diag, causal)
    _unstack_group_heads(_softmax_result(l_sc, acc_sc), e, o_ref)


def _nsa_win_kernel(q_ref, k_ref, v_ref, bias_ref, far_ref, gate_ref, ex_ref, oc_ref, os_ref, o_ref,
                    m_sc, l_sc, acc_sc, ow_sc):
    g = pl.program_id(1)
    qi = pl.program_id(2)
    e = g % 2
    t = q_ref.shape[1]
    rows = NSA_HPG * t
    q4 = _stack_group_heads(q_ref, e).astype(BF16)
    col = lax.broadcasted_iota(jnp.int32, (rows, t), 1)
    row = lax.broadcasted_iota(jnp.int32, (rows, t), 0) % t
    bias_diag = jnp.concatenate([bias_ref[r, 0] for r in range(NSA_HPG)], axis=0)
    bias_near = jnp.concatenate([bias_ref[r, 1] for r in range(NSA_HPG)], axis=0)
    bias_far = jnp.concatenate([jnp.broadcast_to(far_ref[r], (t, t)) for r in range(NSA_HPG)], axis=0)
    _softmax_init(m_sc, l_sc, acc_sc)

    def step(j, bias, mask):
        start = pl.multiple_of(j * t, t)
        k = k_ref[0, pl.ds(start, t), :].astype(BF16)
        v = v_ref[0, pl.ds(start, t), :].astype(BF16)
        _softmax_step(_dot_nt(q4, k) + bias, mask, v, m_sc, l_sc, acc_sc)

    n_back = WINDOW // t
    for d in range(n_back, -1, -1):
        if d == n_back:
            bias, mask = bias_far, col > row
        elif d == 0:
            bias, mask = bias_diag, col <= row
        else:
            bias, mask = (bias_near if d == 1 else bias_far), None

        @pl.when(qi >= d)
        def _(d=d, bias=bias, mask=mask):
            step(qi - d, bias, mask)

    _unstack_group_heads(_softmax_result(l_sc, acc_sc), e, ow_sc)
    sig = 1.0 / (1.0 + jnp.exp(-gate_ref[0]))
    branches = (oc_ref[0], os_ref[0], ow_sc[0])
    out = jnp.zeros(branches[0].shape, F32)
    for br in range(3):
        out = out + _dot_exact_rhs(sig, ex_ref[0, br], parts=3) * branches[br]
    o_ref[0] = out


def _nsa_attention(qn, plain, cmp_pos, cmp_w1, cmp_b1, cmp_w2, k_g, rel_bias):
    b, s, _ = qn.shape
    g, t = NSA_GROUPS, NSA_TQ
    kvw = g * HEAD_DIM
    assert WINDOW % t == 0 and t % SLC_BLOCK == 0 and s % t == 0
    n16 = s // CMP_STRIDE
    n_slc = s // SLC_BLOCK
    width = -(-n_slc // LANES) * LANES
    qblk = (NSA_HPG * HEAD_DIM) // LANES * LANES
    nq = s // t

    cmp = _nsa_compress(plain[..., 0:kvw], plain[..., kvw:2 * kvw], cmp_pos, cmp_w1, cmp_b1, cmp_w2, k_g)

    c_start = np.arange(n16) * CMP_STRIDE
    s_start = np.arange(width) * SLC_BLOCK
    slc_map = ((c_start[:, None] < s_start[None, :] + SLC_BLOCK)
               & (c_start[:, None] + CMP_BLOCK > s_start[None, :])
               & (np.arange(n16)[:, None] < n16 - 1) & (np.arange(width)[None, :] < n_slc))
    slc_map = jnp.asarray(slc_map.astype(np.float32)).astype(BF16)

    q_spec = pl.BlockSpec((1, t, qblk), lambda bi, gi, i: (bi, i, gi))
    o_spec = pl.BlockSpec((1, t, qblk), lambda bi, gi, i: (bi, i, gi))
    o_c, sel = pl.pallas_call(
        functools.partial(_nsa_cmp_kernel, n_slc=n_slc),
        grid=(b, g, nq),
        in_specs=[
            q_spec,
            pl.BlockSpec((1, 1, 1, n16, LANES), lambda bi, gi, i: (0, bi, gi, 0, 0)),
            pl.BlockSpec((1, 1, 1, n16, LANES), lambda bi, gi, i: (1, bi, gi, 0, 0)),
            pl.BlockSpec((n16, width), lambda bi, gi, i: (0, 0)),
        ],
        out_specs=[o_spec, pl.BlockSpec((1, 1, t, width), lambda bi, gi, i: (bi, gi, i, 0))],
        out_shape=[jax.ShapeDtypeStruct((b, s, D_MODEL), F32),
                   jax.ShapeDtypeStruct((b, g, s, width), F32)],
        compiler_params=_params("parallel", "parallel", "parallel"),
        name="nsa_cmp",
    )(qn, cmp, cmp, slc_map)

    tiles, far = _bias_tiles(rel_bias, t)
    bias_spec = pl.BlockSpec((NSA_HPG, 2, t, t), lambda bi, gi, i: (gi, 0, 0, 0))
    far_spec = pl.BlockSpec((NSA_HPG, 1, t), lambda bi, gi, i: (gi, 0, 0))
    scratch = [pltpu.VMEM((NSA_HPG * t, 1), F32), pltpu.VMEM((NSA_HPG * t, 1), F32),
               pltpu.VMEM((NSA_HPG * t, LANES), F32)]
    nqb = D_MODEL // LANES
    ks_spec = pl.BlockSpec((1, s, LANES), lambda bi, gi, i: (bi, 0, nqb + gi // 2))
    kw_spec = pl.BlockSpec((1, s, LANES), lambda bi, gi, i: (bi, 0, nqb + kvw // LANES + gi // 2))
    vs_spec = pl.BlockSpec((1, s, LANES), lambda bi, gi, i: (bi, 0, 2 * kvw // LANES + gi // 2))
    vw_spec = pl.BlockSpec((1, s, LANES), lambda bi, gi, i: (bi, 0, 3 * kvw // LANES + gi // 2))
    gate_spec = pl.BlockSpec((1, t, LANES), lambda bi, gi, i: (bi, i, 4 * kvw // LANES))

    o_s = pl.pallas_call(
        _nsa_sel_kernel,
        grid=(b, g, nq),
        in_specs=[q_spec, pl.BlockSpec((1, 1, t, width), lambda bi, gi, i: (bi, gi, i, 0)),
                  ks_spec, vs_spec, bias_spec, far_spec],
        out_specs=o_spec,
        out_shape=jax.ShapeDtypeStruct((b, s, D_MODEL), F32),
        scratch_shapes=scratch,
        compiler_params=_params("parallel", "parallel", "arbitrary"),
        name="nsa_sel",
    )(qn, sel, qn, plain, tiles, far)

    ex = np.zeros((g, 3, LANES, qblk), np.float32)
    for gi in range(g):
        for r in range(NSA_HPG):
            for br in range(3):
                ex[gi, br, (gi * NSA_HPG + r) * 3 + br, r * HEAD_DIM:(r + 1) * HEAD_DIM] = 1.0
    ex = jnp.asarray(ex).astype(BF16)

    return pl.pallas_call(
        _nsa_win_kernel,
        grid=(b, g, nq),
        in_specs=[q_spec, kw_spec, vw_spec, bias_spec, far_spec, gate_spec,
                  pl.BlockSpec((1, 3, LANES, qblk), lambda bi, gi, i: (gi, 0, 0, 0)),
                  o_spec, o_spec],
        out_specs=o_spec,
        out_shape=jax.ShapeDtypeStruct((b, s, D_MODEL), F32),
        scratch_shapes=scratch + [pltpu.VMEM((1, t, qblk), F32)],
        compiler_params=_params("parallel", "parallel", "arbitrary"),
        name="nsa_win",
    )(qn, qn, plain, tiles, far, plain, ex, o_c, o_s)


def _head_gain(g, n_heads, scale=1.0):
    return jnp.tile(g.astype(F32), n_heads) * scale


def kernel(x, mem, rel_table, attn_norm, mem_norm, w_mem_kv, xq_norm, xk_norm, w_out,
           ffn_norm, w_gate, w_up, w_down, sb_w_in, diff_w_in, diff_q_norm, diff_k_norm,
           diff_lambda, diff_subln, nsa_w_in, nsa_q_norm, nsa_k_norm, nsa_cmp_pos,
           nsa_cmp_w1, nsa_cmp_b1, nsa_cmp_w2, moba_w_in, moba_q_norm, moba_k_norm):
    b, s, d = x.shape
    mlen = mem.shape[1]
    depth = attn_norm.shape[0]
    rel_bias = rel_table.astype(F32)[_t5_bucket(jnp.arange(s))].T
    x2 = x.reshape(b * s, d)
    mem2 = mem.reshape(b * mlen, d)
    dm = D_MODEL
    for i in range(depth):
        kind, j = i % 4, i // 4
        xq_gain = _head_gain(xq_norm[i], X_HEADS, SCALE)
        if kind == 0:
            w = sb_w_in[j]
            w_plain = jnp.concatenate([w[:, :dm] * SCALE, w[:, dm:3 * dm]], axis=1).astype(BF16)
            plain = _project(x2, attn_norm[i], w_plain).reshape(b, s, -1)
            xq = _project(x2, attn_norm[i], w[:, 3 * dm:].astype(BF16), xq_gain).reshape(b, s, -1)
            mix = _sb_attention(plain)
        elif kind == 1 or kind == 3:
            w, qg, kg = ((diff_w_in[j], diff_q_norm[j], diff_k_norm[j]) if kind == 1
                         else (moba_w_in[j], moba_q_norm[j], moba_k_norm[j]))
            w_norm = jnp.concatenate([w[:, :2 * dm], w[:, 3 * dm:]], axis=1).astype(BF16)
            gains = jnp.concatenate([_head_gain(qg, N_HEADS, SCALE), _head_gain(kg, N_HEADS), xq_gain])
            normed = _project(x2, attn_norm[i], w_norm, gains).reshape(b, s, -1)
            v = _project(x2, attn_norm[i], w[:, 2 * dm:3 * dm].astype(BF16)).reshape(b, s, -1)
            xq = normed[..., 2 * dm:]
            if kind == 1:
                lv = diff_lambda[j].astype(F32)
                lam_init = 0.8 - 0.6 * math.exp(-0.3 * i)
                lam = jnp.exp(jnp.sum(lv[0] * lv[1])) - jnp.exp(jnp.sum(lv[2] * lv[3])) + lam_init
                mix = _diff_attention(normed, v, lam, diff_subln[j], rel_bias, 1.0 - lam_init)
            else:
                mix = _moba_attention(normed, v, rel_bias)
        else:
            w = nsa_w_in[j]
            kvw = NSA_GROUPS * HEAD_DIM
            offs = np.cumsum([0, dm, kvw, kvw, kvw, kvw, kvw, kvw, 3 * N_HEADS, X_WIDTH])
            seg = [w[:, int(offs[a]):int(offs[a + 1])] for a in range(9)]
            gates_w = jnp.pad(seg[7], ((0, 0), (0, LANES - 3 * N_HEADS)))
            w_norm = jnp.concatenate([seg[0], seg[3], seg[5], seg[8]], axis=1).astype(BF16)
            w_plain = jnp.concatenate([seg[1], seg[2], seg[4], seg[6], gates_w], axis=1).astype(BF16)
            n_kv_heads = NSA_GROUPS
            gains = jnp.concatenate([_head_gain(nsa_q_norm[j], N_HEADS, SCALE),
                                     _head_gain(nsa_k_norm[j], 2 * n_kv_heads), xq_gain])
            normed = _project(x2, attn_norm[i], w_norm, gains).reshape(b, s, -1)
            plain = _project(x2, attn_norm[i], w_plain).reshape(b, s, -1)
            xq = normed[..., dm + 2 * kvw:]
            mix = _nsa_attention(normed, plain, nsa_cmp_pos[j], nsa_cmp_w1[j], nsa_cmp_b1[j],
                                 nsa_cmp_w2[j], nsa_k_norm[j], rel_bias)
        wm = w_mem_kv[i]
        mk = _project(mem2, mem_norm[i], wm[:, :X_WIDTH].astype(BF16),
                      _head_gain(xk_norm[i], X_HEADS)).reshape(b, mlen, -1)
        mv = _project(mem2, mem_norm[i], wm[:, X_WIDTH:].astype(BF16)).reshape(b, mlen, -1)
        cross = _cross_attention(xq, mk, mv)
        x2 = _out_project(x2, mix.reshape(b * s, dm), cross.reshape(b * s, X_WIDTH), w_out[i])
        x2 = _ffn(x2, ffn_norm[i], w_gate[i], w_up[i], w_down[i])
    return x2.reshape(b, s, d)
```
